```python
import math
import jax, jax.numpy as jnp
from jax import lax
import numpy as np

D_MODEL = 1024
BATCH = 16
SEQ = 4096
DEPTH = 2
DEC_BATCH = 8
DEC_SEQ = 8192
PAST_LEN = 128

BLOCK = 128
EPS = 1e-6
A_HEADS = 4
A_HEAD_DIM = 64
A_VDIM = 2 * A_HEAD_DIM
A_QK = A_HEADS * 2 * A_HEAD_DIM
A_WIDTH = A_HEADS * A_VDIM
B_HEADS = 8
B_KV_HEADS = 2
B_GROUP = B_HEADS // B_KV_HEADS
B_HEAD_DIM = 64
B_WIDTH = B_HEADS * B_HEAD_DIM
B_KV = B_KV_HEADS * B_HEAD_DIM
WINDOW = 128
D_FF = 2816
CONV_WIDTH = 3
SPLIT_SIZES = (A_QK, A_QK, A_WIDTH, B_WIDTH, B_KV, B_KV, D_MODEL, D_MODEL)
SPLIT_IDX = tuple(int(i) for i in np.cumsum(SPLIT_SIZES)[:-1])
IN_COLS = int(sum(SPLIT_SIZES))
NEG = -1e30

kernel_name = "hybrid_diffattn_swa_sink_convffn_encoder"


def rmsnorm(x, g):
    xf = x.astype(jnp.float32)
    y = xf * lax.rsqrt(jnp.mean(xf * xf, axis=-1, keepdims=True) + EPS)
    return (y * g.astype(jnp.float32)).astype(x.dtype)


def alibi_slopes(n):
    return jnp.asarray([2.0 ** (-8.0 * (h + 1) / n) for h in range(n)], dtype=jnp.float32)


def diff_attention(q, k, v, lam, lam_init, sub_g):
    B, S = q.shape[0], q.shape[1]
    nb = S // BLOCK
    scale = A_HEAD_DIM ** -0.5
    slopes = alibi_slopes(A_HEADS)[:, None, None, None]
    kpos = jnp.arange(S, dtype=jnp.float32)
    qb = q.reshape(B, nb, BLOCK, A_HEADS, 2, A_HEAD_DIM).transpose(1, 0, 2, 3, 4, 5)

    def one_block(args):
        qi, i = args
        s = jnp.einsum('bqhcd,bkhcd->bhcqk', qi, k).astype(jnp.float32) * scale
        qpos = (i * BLOCK + jnp.arange(BLOCK)).astype(jnp.float32)
        dist = jnp.abs(qpos[:, None] - kpos[None, :])
        p = jax.nn.softmax(s - slopes * dist, axis=-1)
        a = p[:, :, 0] - lam * p[:, :, 1]
        return jnp.einsum('bhqk,bkhe->bqhe', a.astype(v.dtype), v)

    o = lax.map(one_block, (qb, jnp.arange(nb)))
    o = o.transpose(1, 0, 2, 3, 4).reshape(B, S, A_HEADS, A_VDIM)
    o = rmsnorm(o, sub_g) * (1.0 - lam_init)
    return o.reshape(B, S, A_WIDTH)


def window_attention(q, k, v, sink):
    B, S = q.shape[0], q.shape[1]
    nb = S // BLOCK
    scale = B_HEAD_DIM ** -0.5
    kp = jnp.pad(k, ((0, 0), (BLOCK, BLOCK), (0, 0), (0, 0)))
    vp = jnp.pad(v, ((0, 0), (BLOCK, BLOCK), (0, 0), (0, 0)))
    qb = q.reshape(B, nb, BLOCK, B_KV_HEADS, B_GROUP, B_HEAD_DIM).transpose(1, 0, 2, 3, 4, 5)
    slopes = alibi_slopes(B_HEADS).reshape(B_KV_HEADS, B_GROUP)[:, :, None, None]
    sink_g = sink.astype(jnp.float32).reshape(B_KV_HEADS, B_GROUP)[:, :, None, None]
    qi_idx = jnp.arange(BLOCK)
    kj_idx = jnp.arange(3 * BLOCK)
    rel = kj_idx[None, :] - qi_idx[:, None]
    band = (rel >= BLOCK - WINDOW) & (rel <= BLOCK + WINDOW)
    dist = jnp.abs(rel - BLOCK).astype(jnp.float32)

    def one_block(args):
        qi, i = args
        kw = lax.dynamic_slice_in_dim(kp, i * BLOCK, 3 * BLOCK, axis=1)
        vw = lax.dynamic_slice_in_dim(vp, i * BLOCK, 3 * BLOCK, axis=1)
        kpos = (i - 1) * BLOCK + kj_idx
        valid = band & ((kpos >= 0) & (kpos < S))[None, :]
        s = jnp.einsum('bqgrd,bkgd->bgrqk', qi, kw).astype(jnp.float32) * scale - slopes * dist
        s = jnp.where(valid, s, NEG)
        m = jnp.maximum(jnp.max(s, axis=-1, keepdims=True), sink_g)
        e = jnp.exp(s - m)
        p = e / (jnp.sum(e, axis=-1, keepdims=True) + jnp.exp(sink_g - m))
        return jnp.einsum('bgrqk,bkgd->bqgrd', p.astype(v.dtype), vw)

    o = lax.map(one_block, (qb, jnp.arange(nb)))
    return o.transpose(1, 0, 2, 3, 4, 5).reshape(B, S, B_WIDTH)


def dwconv_centred(a, w, b):
    ap = jnp.pad(a, ((0, 0), (1, 1), (0, 0)))
    return ap[:, :-2] * w[0] + ap[:, 1:-1] * w[1] + ap[:, 2:] * w[2] + b


def trunk_layer(x, l, attn_norm, w_in, gate_bias, lambda_q1, lambda_k1, lambda_q2, lambda_k2,
                subln, sink, w_proj_a, w_proj_b, w_out, ffn_norm, w_up, conv_w, conv_b, w_down):
    B, S = x.shape[0], x.shape[1]
    lam_init = 0.8 - 0.6 * math.exp(-0.3 * l)
    h = rmsnorm(x, attn_norm[l])
    proj = h @ w_in[l]
    qa, ka, va, qb, kb, vb, ga, gb = jnp.split(proj, SPLIT_IDX, axis=-1)
    lam = (jnp.exp(jnp.sum(lambda_q1[l].astype(jnp.float32) * lambda_k1[l].astype(jnp.float32)))
           - jnp.exp(jnp.sum(lambda_q2[l].astype(jnp.float32) * lambda_k2[l].astype(jnp.float32)))
           + lam_init)
    ya = diff_attention(qa.reshape(B, S, A_HEADS, 2, A_HEAD_DIM),
                        ka.reshape(B, S, A_HEADS, 2, A_HEAD_DIM),
                        va.reshape(B, S, A_HEADS, A_VDIM), lam, lam_init, subln[l])
    yb = window_attention(qb.reshape(B, S, B_KV_HEADS, B_GROUP, B_HEAD_DIM),
                          kb.reshape(B, S, B_KV_HEADS, B_HEAD_DIM),
                          vb.reshape(B, S, B_KV_HEADS, B_HEAD_DIM), sink[l])
    bga, bgb = jnp.split(gate_bias[l], 2)
    merged = (jax.nn.sigmoid(ga + bga) * (ya @ w_proj_a[l])
              + jax.nn.sigmoid(gb + bgb) * (yb @ w_proj_b[l]))
    x = x + merged @ w_out[l]
    h = rmsnorm(x, ffn_norm[l])
    a, v = jnp.split(h @ w_up[l], 2, axis=-1)
    a = dwconv_centred(a, conv_w[l], conv_b[l])
    return x + (jax.nn.gelu(a) * v) @ w_down[l]


def setup_inputs(seed: int = 0) -> dict:
    key = jax.random.key(seed)
    ks = jax.random.split(key, 20)
    f32 = jnp.float32
    nrm = lambda k, shp, s: jax.random.normal(k, shp, f32) * s
    return {
        "x_prompt": nrm(ks[0], (BATCH, SEQ, D_MODEL), 1.0),
        "x_sample": nrm(ks[1], (DEC_BATCH, DEC_SEQ, D_MODEL), 1.0),
        "attn_norm": 1.0 + nrm(ks[2], (DEPTH, D_MODEL), 0.02),
        "w_in": nrm(ks[3], (DEPTH, D_MODEL, IN_COLS), D_MODEL ** -0.5),
        "gate_bias": nrm(ks[4], (DEPTH, 2 * D_MODEL), 0.02),
        "lambda_q1": nrm(ks[5], (DEPTH, A_HEAD_DIM), 0.1),
        "lambda_k1": nrm(ks[6], (DEPTH, A_HEAD_DIM), 0.1),
        "lambda_q2": nrm(ks[7], (DEPTH, A_HEAD_DIM), 0.1),
        "lambda_k2": nrm(ks[8], (DEPTH, A_HEAD_DIM), 0.1),
        "subln": 1.0 + nrm(ks[9], (DEPTH, A_VDIM), 0.02),
        "sink": nrm(ks[10], (DEPTH, B_HEADS), 1.0),
        "w_proj_a": nrm(ks[11], (DEPTH, A_WIDTH, D_MODEL), A_WIDTH ** -0.5),
        "w_proj_b": nrm(ks[12], (DEPTH, B_WIDTH, D_MODEL), B_WIDTH ** -0.5),
        "w_out": nrm(ks[13], (DEPTH, D_MODEL, D_MODEL), D_MODEL ** -0.5),
        "ffn_norm": 1.0 + nrm(ks[14], (DEPTH, D_MODEL), 0.02),
        "w_up": nrm(ks[15], (DEPTH, D_MODEL, 2 * D_FF), D_MODEL ** -0.5),
        "conv_w": nrm(ks[16], (DEPTH, CONV_WIDTH, D_FF), 0.5),
        "conv_b": nrm(ks[17], (DEPTH, D_FF), 0.02),
        "w_down": nrm(ks[18], (DEPTH, D_FF, D_MODEL), D_FF ** -0.5),
        "final_norm": 1.0 + nrm(ks[19], (D_MODEL,), 0.02),
    }


def reference(x_prompt, x_sample, attn_norm, w_in, gate_bias, lambda_q1, lambda_k1, lambda_q2,
              lambda_k2, subln, sink, w_proj_a, w_proj_b, w_out, ffn_norm, w_up, conv_w, conv_b,
              w_down, final_norm):
    xp = x_prompt
    xs = x_sample
    for l in range(DEPTH):
        xp = trunk_layer(xp, l, attn_norm, w_in, gate_bias, lambda_q1, lambda_k1, lambda_q2,
                         lambda_k2, subln, sink, w_proj_a, w_proj_b, w_out, ffn_norm, w_up,
                         conv_w, conv_b, w_down)
        xs = trunk_layer(xs, l, attn_norm, w_in, gate_bias, lambda_q1, lambda_k1, lambda_q2,
                         lambda_k2, subln, sink, w_proj_a, w_proj_b, w_out, ffn_norm, w_up,
                         conv_w, conv_b, w_down)
    y_prompt = rmsnorm(xp, final_norm)
    y_sample = rmsnorm(xs, final_norm)
    return (y_prompt, y_sample)
```

```python
import functools
import math

import jax
import jax.numpy as jnp
from jax import lax
from jax.experimental import pallas as pl
from jax.experimental.pallas import tpu as pltpu

F32 = jnp.float32
BF16 = jnp.bfloat16

D_MODEL = 1024
DEPTH = 2
EPS = 1e-6
HEAD_DIM = 64
A_HEADS = 4
A_VDIM = 128
B_HEADS = 8
B_KV_HEADS = 2
WINDOW = 128
D_FF = 2816
NEG = -1e30
LOG2E = math.log2(math.e)
Q_SCALE = HEAD_DIM ** -0.5 * LOG2E

LANES = 128
SUBLANES = 8
VMEM_LIMIT = 56 * 1024 * 1024

N_EXT = 36 * LANES
PROJ_CHUNK = 512
GATE_CHUNKS = 2 * D_MODEL // PROJ_CHUNK
Q_CHUNKS = (GATE_CHUNKS, GATE_CHUNKS + 3)
COL_QA, COL_KA, COL_VA = 16, 20, 24

TM = 512
TQ = 512
TK = 512
BQ = 128
BW = 3 * BQ
FC = 256
HALO = SUBLANES


def _rms(x, g):
    return x * lax.rsqrt(jnp.mean(x * x, axis=-1, keepdims=True) + EPS) * g


def _const_spec(shape):
    nd = len(shape)
    return pl.BlockSpec(shape, lambda *_: (0,) * nd, pipeline_mode=pl.Buffered(1))


def _inproj_kernel(x_ref, g_ref, w_ref, gb_ref, o_ref):
    h = _rms(x_ref[...], g_ref[...]).astype(BF16)
    for c in range(N_EXT // PROJ_CHUNK):
        cols = slice(c * PROJ_CHUNK, (c + 1) * PROJ_CHUNK)
        y = jnp.dot(h, w_ref[:, cols], preferred_element_type=F32)
        if c in Q_CHUNKS:
            y = y * Q_SCALE
        elif c < GATE_CHUNKS:
            y = jax.nn.sigmoid(y + gb_ref[:, cols])
        o_ref[:, cols] = y.astype(BF16)


def _inproj(x, g, w_ext, gate_bias):
    t = x.shape[0]
    return pl.pallas_call(
        _inproj_kernel,
        grid=(t // TM,),
        in_specs=[
            pl.BlockSpec((TM, D_MODEL), lambda i: (i, 0)),
            _const_spec((1, D_MODEL)),
            _const_spec((D_MODEL, N_EXT)),
            _const_spec((1, 2 * D_MODEL)),
        ],
        out_specs=pl.BlockSpec((TM, N_EXT), lambda i: (i, 0)),
        out_shape=jax.ShapeDtypeStruct((t, N_EXT), BF16),
        compiler_params=pltpu.CompilerParams(
            dimension_semantics=("arbitrary",), vmem_limit_bytes=VMEM_LIMIT),
        name="in_proj",
    )(x, g, w_ext, gate_bias)


def _attn_a_kernel(c_ref, lam_ref, sub_ref, dbase_ref, q_ref, k_ref, v_ref, o_ref, *, n_kt, lam_init):
    head = pl.program_id(1)
    qi = pl.program_id(2)
    c = c_ref[head]
    q = q_ref[...].astype(F32)
    lane = lax.broadcasted_iota(jnp.int32, q.shape, 1)
    qm = (jnp.where(lane < HEAD_DIM, q, 0.0).astype(BF16),
          jnp.where(lane >= HEAD_DIM, q, 0.0).astype(BF16))
    col = lax.broadcasted_iota(jnp.int32, (1, TK), 1).astype(F32)
    nt = (((1,), (1,)), ((), ()))

    def tile(kt, carry, mode):
        ks = pl.multiple_of(kt * TK, TK)
        k = k_ref[pl.ds(ks, TK), :]
        v = v_ref[pl.ds(ks, TK), :]
        k0 = (kt * TK).astype(F32)
        if mode == "left":
            bias = c * (col + k0)
        elif mode == "right":
            bias = -c * (col + k0)
        else:
            bias = c * dbase_ref[...] + c * k0
        new = []
        for mp in range(2):
            m, l, acc = carry[mp]
            u = lax.dot_general(qm[mp], k, nt, preferred_element_type=F32) + bias
            m_new = jnp.maximum(m, jnp.max(u, axis=-1, keepdims=True))
            alpha = jnp.exp2(m - m_new)
            p = jnp.exp2(u - m_new)
            l = alpha * l + jnp.sum(p, axis=-1, keepdims=True)
            acc = alpha * acc + jnp.dot(p.astype(BF16), v, preferred_element_type=F32)
            new.append((m_new, l, acc))
        return tuple(new)

    init_one = (jnp.full((TQ, 1), NEG, F32), jnp.zeros((TQ, 1), F32), jnp.zeros((TQ, A_VDIM), F32))
    carry = (init_one, init_one)
    carry = lax.fori_loop(0, qi, functools.partial(tile, mode="left"), carry)
    carry = tile(qi, carry, "diag")
    qpos = (qi * TQ).astype(F32) + lax.broadcasted_iota(jnp.int32, (TQ, 1), 0).astype(F32)
    shift = 2.0 * c * qpos
    carry = tuple((m - shift, l, acc) for (m, l, acc) in carry)
    carry = lax.fori_loop(qi + 1, n_kt, functools.partial(tile, mode="right"), carry)

    (_, l0, a0), (_, l1, a1) = carry
    lv = lam_ref[...]
    lam = (jnp.exp(jnp.sum(lv[0:1] * lv[1:2], axis=-1, keepdims=True))
           - jnp.exp(jnp.sum(lv[2:3] * lv[3:4], axis=-1, keepdims=True)) + lam_init)
    o = a0 * (1.0 / l0) - lam * (a1 * (1.0 / l1))
    o_ref[...] = (_rms(o, sub_ref[...]) * (1.0 - lam_init)).astype(BF16)


def _attn_a(proj3, c_slopes, lam_vecs, subln, dbase, lam_init):
    b, s, _ = proj3.shape
    n_kt = s // TK
    kern = functools.partial(_attn_a_kernel, n_kt=n_kt, lam_init=lam_init)
    return pl.pallas_call(
        kern,
        grid=(b, A_HEADS, s // TQ),
        in_specs=[
            pl.BlockSpec(memory_space=pltpu.SMEM),
            _const_spec((4, HEAD_DIM)),
            _const_spec((1, A_VDIM)),
            _const_spec((TQ, TK)),
            pl.BlockSpec((None, TQ, LANES), lambda bi, h, qi: (bi, qi, COL_QA + h)),
            pl.BlockSpec((None, s, LANES), lambda bi, h, qi: (bi, 0, COL_KA + h)),
            pl.BlockSpec((None, s, LANES), lambda bi, h, qi: (bi, 0, COL_VA + h)),
        ],
        out_specs=pl.BlockSpec((None, TQ, LANES), lambda bi, h, qi: (bi, qi, h)),
        out_shape=jax.ShapeDtypeStruct((b, s, A_HEADS * A_VDIM), BF16),
        compiler_params=pltpu.CompilerParams(
            dimension_semantics=("arbitrary", "arbitrary", "arbitrary"),
            vmem_limit_bytes=VMEM_LIMIT),
        name="attn_a",
    )(c_slopes, lam_vecs, subln, dbase, proj3, proj3, proj3)


def _attn_b_kernel(c_ref, sink_ref, q_ref, k_ref, v_ref, o_ref, *, s_len):
    qi = pl.program_id(1)
    start = pl.multiple_of(jnp.clip((qi - 1) * BQ, 0, s_len - BW), BQ)
    kw = k_ref[pl.ds(start, BW), :]
    vw = v_ref[pl.ds(start, BW), :]
    rel = ((start - qi * BQ) + lax.broadcasted_iota(jnp.int32, (BQ, BW), 1)
           - lax.broadcasted_iota(jnp.int32, (BQ, BW), 0))
    dist = jnp.abs(rel)
    valid = dist <= WINDOW
    distf = dist.astype(F32)
    q = q_ref[...].astype(F32)
    lane = lax.broadcasted_iota(jnp.int32, (BQ, LANES), 1)
    low = lane < HEAD_DIM
    nt = (((1,), (1,)), ((), ()))
    outs = []
    for colblk in range(B_HEADS // 2):
        qv = q[:, colblk * LANES:(colblk + 1) * LANES]
        pair = []
        for half in range(2):
            hh = 2 * colblk + half
            g = hh // (B_HEADS // B_KV_HEADS)
            qmask = low if half == 0 else jnp.logical_not(low)
            qh = jnp.where(qmask, qv, 0.0).astype(BF16)
            kg = kw[:, g * LANES:(g + 1) * LANES]
            vg = vw[:, g * LANES:(g + 1) * LANES]
            sc = lax.dot_general(qh, kg, nt, preferred_element_type=F32)
            sc = jnp.where(valid, sc - c_ref[hh] * distf, NEG)
            sk = sink_ref[hh]
            m = jnp.maximum(jnp.max(sc, axis=-1, keepdims=True), sk)
            e = jnp.exp2(sc - m)
            den = jnp.sum(e, axis=-1, keepdims=True) + jnp.exp2(sk - m)
            pair.append(jnp.dot(e.astype(BF16), vg, preferred_element_type=F32) * (1.0 / den))
        outs.append(jnp.where(low, pair[0], pair[1]))
    o_ref[...] = jnp.concatenate(outs, axis=1).astype(BF16)


def _attn_b(proj3, c_slopes, sink2):
    b, s, _ = proj3.shape
    kern = functools.partial(_attn_b_kernel, s_len=s)
    return pl.pallas_call(
        kern,
        grid=(b, s // BQ),
        in_specs=[
            pl.BlockSpec(memory_space=pltpu.SMEM),
            pl.BlockSpec(memory_space=pltpu.SMEM),
            pl.BlockSpec((None, BQ, 4 * LANES), lambda bi, qi: (bi, qi, 7)),
            pl.BlockSpec((None, s, 2 * LANES), lambda bi, qi: (bi, 0, 16)),
            pl.BlockSpec((None, s, 2 * LANES), lambda bi, qi: (bi, 0, 17)),
        ],
        out_specs=pl.BlockSpec((None, BQ, 4 * LANES), lambda bi, qi: (bi, qi, 0)),
        out_shape=jax.ShapeDtypeStruct((b, s, B_HEADS * HEAD_DIM), BF16),
        compiler_params=pltpu.CompilerParams(
            dimension_semantics=("arbitrary", "arbitrary"), vmem_limit_bytes=VMEM_LIMIT),
        name="attn_b",
    )(c_slopes, sink2, proj3, proj3, proj3)


def _merge_kernel(x_ref, ya_ref, yb_ref, sa_ref, sb_ref, wa_ref, wb_ref, wo_ref, o_ref):
    pa = jnp.dot(ya_ref[...], wa_ref[...], preferred_element_type=F32)
    pb = jnp.dot(yb_ref[...], wb_ref[...], preferred_element_type=F32)
    merged = sa_ref[...].astype(F32) * pa + sb_ref[...].astype(F32) * pb
    o_ref[...] = x_ref[...] + jnp.dot(merged.astype(BF16), wo_ref[...], preferred_element_type=F32)


def _merge(x, ya, yb, proj, wa, wb, wo):
    t = x.shape[0]
    return pl.pallas_call(
        _merge_kernel,
        grid=(t // TM,),
        in_specs=[
            pl.BlockSpec((TM, D_MODEL), lambda i: (i, 0)),
            pl.BlockSpec((TM, 512), lambda i: (i, 0)),
            pl.BlockSpec((TM, 512), lambda i: (i, 0)),
            pl.BlockSpec((TM, D_MODEL), lambda i: (i, 0)),
            pl.BlockSpec((TM, D_MODEL), lambda i: (i, 1)),
            _const_spec((512, D_MODEL)),
            _const_spec((512, D_MODEL)),
            _const_spec((D_MODEL, D_MODEL)),
        ],
        out_specs=pl.BlockSpec((TM, D_MODEL), lambda i: (i, 0)),
        out_shape=jax.ShapeDtypeStruct((t, D_MODEL), F32),
        compiler_params=pltpu.CompilerParams(
            dimension_semantics=("arbitrary",), vmem_limit_bytes=VMEM_LIMIT),
        name="merge",
    )(x, ya, yb, proj, proj, wa, wb, wo)


def _ffn_kernel(xp_ref, x_ref, xn_ref, g_ref, wu_ref, cw_ref, cb_ref, wd_ref, fin_ref, o_ref,
                *, tiles_per_seq, final):
    j = pl.program_id(0) % tiles_per_seq
    g = g_ref[...]
    x = x_ref[...]
    hp = _rms(xp_ref[...], g) * (j > 0).astype(F32)
    hn = _rms(xn_ref[...], g) * (j < tiles_per_seq - 1).astype(F32)
    hm = _rms(x, g)
    h_ext = jnp.concatenate([hp, hm, hn], axis=0).astype(BF16)
    h_main = hm.astype(BF16)
    rows = TM + 2 * HALO
    acc = None
    for c in range(D_FF // FC):
        cols = slice(c * FC, (c + 1) * FC)
        a = jnp.dot(h_ext, wu_ref[:, cols], preferred_element_type=F32)
        vv = jnp.dot(h_main, wu_ref[:, D_FF + c * FC:D_FF + (c + 1) * FC], preferred_element_type=F32)
        a_prev = pltpu.roll(a, 1, 0)[HALO:HALO + TM]
        a_next = pltpu.roll(a, rows - 1, 0)[HALO:HALO + TM]
        a_cur = a[HALO:HALO + TM]
        ac = (a_prev * cw_ref[0:1, cols] + a_cur * cw_ref[1:2, cols] + a_next * cw_ref[2:3, cols]
              + cb_ref[:, cols])
        gated = (jax.nn.gelu(ac) * vv).astype(BF16)
        d = jnp.dot(gated, wd_ref[cols, :], preferred_element_type=F32)
        acc = d if acc is None else acc + d
    y = x + acc
    if final:
        y = _rms(y, fin_ref[...])
    o_ref[...] = y


def _ffn(x, s_len, g, wu, cw, cb, wd, fin, final):
    t = x.shape[0]
    tiles_per_seq = s_len // TM
    hb = TM // HALO
    n_hb = t // HALO
    kern = functools.partial(_ffn_kernel, tiles_per_seq=tiles_per_seq, final=final)
    return pl.pallas_call(
        kern,
        grid=(t // TM,),
        in_specs=[
            pl.BlockSpec((HALO, D_MODEL), lambda i: (jnp.maximum(i * hb - 1, 0), 0)),
            pl.BlockSpec((TM, D_MODEL), lambda i: (i, 0)),
            pl.BlockSpec((HALO, D_MODEL), lambda i: (jnp.minimum((i + 1) * hb, n_hb - 1), 0)),
            _const_spec((1, D_MODEL)),
            _const_spec((D_MODEL, 2 * D_FF)),
            _const_spec((3, D_FF)),
            _const_spec((1, D_FF)),
            _const_spec((D_FF, D_MODEL)),
            _const_spec((1, D_MODEL)),
        ],
        out_specs=pl.BlockSpec((TM, D_MODEL), lambda i: (i, 0)),
        out_shape=jax.ShapeDtypeStruct((t, D_MODEL), F32),
        compiler_params=pltpu.CompilerParams(
            dimension_semantics=("arbitrary",), vmem_limit_bytes=VMEM_LIMIT),
        name="ffn",
    )(x, x, x, g, wu, cw, cb, wd, fin)


def _alibi(n):
    return [2.0 ** (-8.0 * (h + 1) / n) for h in range(n)]


def _extend_w_in(w):
    qa, ka, va, qb = w[:, 0:512], w[:, 512:1024], w[:, 1024:1536], w[:, 1536:2048]
    k0, k1 = w[:, 2048:2112], w[:, 2112:2176]
    v0, v1 = w[:, 2176:2240], w[:, 2240:2304]
    gates = w[:, 2304:]
    return jnp.concatenate([gates, qa, ka, va, qb, k0, k0, k1, k1, v0, v0, v1, v1], axis=1).astype(BF16)


def _layer(x, b, s, l, p, dbase, final):
    lam_init = 0.8 - 0.6 * math.exp(-0.3 * l)
    proj = _inproj(x, p["attn_norm"][l][None], _extend_w_in(p["w_in"][l]), p["gate_bias"][l][None])
    proj3 = proj.reshape(b, s, N_EXT)
    c_a = jnp.asarray(_alibi(A_HEADS), F32) * LOG2E
    c_b = jnp.asarray(_alibi(B_HEADS), F32) * LOG2E
    lam_vecs = jnp.stack([p["lambda_q1"][l], p["lambda_k1"][l], p["lambda_q2"][l], p["lambda_k2"][l]])
    ya = _attn_a(proj3, c_a, lam_vecs, p["subln"][l][None], dbase, lam_init)
    yb = _attn_b(proj3, c_b, p["sink"][l].astype(F32) * LOG2E)
    t = b * s
    x1 = _merge(x, ya.reshape(t, 512), yb.reshape(t, 512), proj,
                p["w_proj_a"][l].astype(BF16), p["w_proj_b"][l].astype(BF16), p["w_out"][l].astype(BF16))
    return _ffn(x1, s, p["ffn_norm"][l][None], p["w_up"][l].astype(BF16), p["conv_w"][l],
                p["conv_b"][l][None], p["w_down"][l].astype(BF16), p["final_norm"][None], final)


def _trunk(x3, p, dbase):
    b, s, _ = x3.shape
    x = x3.reshape(b * s, D_MODEL)
    for l in range(DEPTH):
        x = _layer(x, b, s, l, p, dbase, final=(l == DEPTH - 1))
    return x.reshape(b, s, D_MODEL)


def kernel(x_prompt, x_sample, attn_norm, w_in, gate_bias, lambda_q1, lambda_k1, lambda_q2, lambda_k2,
           subln, sink, w_proj_a, w_proj_b, w_out, ffn_norm, w_up, conv_w, conv_b, w_down, final_norm):
    p = dict(attn_norm=attn_norm, w_in=w_in, gate_bias=gate_bias, lambda_q1=lambda_q1, lambda_k1=lambda_k1,
             lambda_q2=lambda_q2, lambda_k2=lambda_k2, subln=subln, sink=sink, w_proj_a=w_proj_a,
             w_proj_b=w_proj_b, w_out=w_out, ffn_norm=ffn_norm, w_up=w_up, conv_w=conv_w, conv_b=conv_b,
             w_down=w_down, final_norm=final_norm)
    i = lax.broadcasted_iota(jnp.int32, (TQ, TK), 0)
    j = lax.broadcasted_iota(jnp.int32, (TQ, TK), 1)
    dbase = (i - jnp.abs(i - j)).astype(F32)
    return _trunk(x_prompt, p, dbase), _trunk(x_sample, p, dbase)
```
